```python
import math
import jax, jax.numpy as jnp
from jax import lax
import numpy as np

D_MODEL = 1024
BATCH = 2
SEQ = 8192
DEPTH = 2

N_MIXERS = 2
RMS_EPS = 1e-6
ROPE_THETA = 10000.0
Q_BLOCK = 128
MAX_POS_OFFSET = 4096

DA_HEADS = 8
DA_QK_DIM = 64
DA_V_DIM = 2 * DA_QK_DIM
DA_WIDTH = DA_HEADS * DA_V_DIM

MLA_HEADS = 8
MLA_Q_LORA = 384
MLA_KV_LORA = 256
MLA_NOPE = 128
MLA_ROPE = 64
MLA_V = 128
MLA_WIDTH = MLA_HEADS * MLA_V
MLA_IN = MLA_Q_LORA + MLA_KV_LORA + MLA_ROPE + MLA_WIDTH

kernel_name = "hybrid_diffattn_mla_adaln_encoder"


def _rmsnorm(x, g):
    xf = x.astype(jnp.float32)
    y = xf * lax.rsqrt(jnp.mean(xf * xf, axis=-1, keepdims=True) + RMS_EPS)
    return (y * g.astype(jnp.float32)).astype(x.dtype)


def _rope(x, positions):
    d = x.shape[-1]
    half = d // 2
    inv = ROPE_THETA ** (-jnp.arange(half, dtype=jnp.float32) / half)
    ang = positions.astype(jnp.float32)[..., None] * inv
    cos = jnp.cos(ang)[:, :, None, :]
    sin = jnp.sin(ang)[:, :, None, :]
    xf = x.astype(jnp.float32)
    x1, x2 = xf[..., :half], xf[..., half:]
    out = jnp.concatenate([x1 * cos - x2 * sin, x2 * cos + x1 * sin], axis=-1)
    return out.astype(x.dtype)


def _to_blocks(t):
    b, s = t.shape[:2]
    t = t.reshape((b, s // Q_BLOCK, Q_BLOCK) + t.shape[2:])
    return jnp.moveaxis(t, 1, 0)


def _from_blocks(t):
    t = jnp.moveaxis(t, 0, 1)
    return t.reshape((t.shape[0], t.shape[1] * t.shape[2]) + t.shape[3:])


def _lambda_init(layer_idx):
    return 0.8 - 0.6 * math.exp(-0.3 * layer_idx)


def _modulation(c, ada_w, ada_b):
    mod = jax.nn.silu(c) @ ada_w + ada_b
    shift, scale, gate = jnp.split(mod[:, None, :], 3, axis=-1)
    return shift, scale, gate


def _diff_attention(h, positions, w_in, lam_q1, lam_k1, lam_q2, lam_k2, subln_g, w_out, lambda_init):
    b, s, _ = h.shape
    proj = h @ w_in
    q, k, v, z = jnp.split(proj, [DA_WIDTH, 2 * DA_WIDTH, 3 * DA_WIDTH], axis=-1)
    q = _rope(q.reshape(b, s, 2 * DA_HEADS, DA_QK_DIM), positions)
    k = _rope(k.reshape(b, s, 2 * DA_HEADS, DA_QK_DIM), positions)
    q = q.reshape(b, s, DA_HEADS, 2, DA_QK_DIM)
    k = k.reshape(b, s, DA_HEADS, 2, DA_QK_DIM)
    v = v.reshape(b, s, DA_HEADS, DA_V_DIM)
    f32 = jnp.float32
    lam = (jnp.exp(jnp.sum(lam_q1.astype(f32) * lam_k1.astype(f32)))
           - jnp.exp(jnp.sum(lam_q2.astype(f32) * lam_k2.astype(f32))) + lambda_init)
    scale = DA_QK_DIM ** -0.5

    def block(qb):
        sc = jnp.einsum('bqhcd,bkhcd->bhcqk', qb, k, preferred_element_type=f32) * scale
        p = jax.nn.softmax(sc, axis=-1)
        p = p[:, :, 0] - lam * p[:, :, 1]
        return jnp.einsum('bhqk,bkhd->bqhd', p.astype(v.dtype), v)

    o = _from_blocks(lax.map(block, _to_blocks(q)))
    o = _rmsnorm(o, subln_g) * (1.0 - lambda_init)
    o = o.reshape(b, s, DA_WIDTH) * jax.nn.silu(z)
    return o @ w_out


def _mla(h, positions, w_in, q_a_norm_g, w_q_b, kv_a_norm_g, w_kv_b, w_out):
    b, s, _ = h.shape
    proj = h @ w_in
    q_a, c_kv, k_pe, z = jnp.split(
        proj, [MLA_Q_LORA, MLA_Q_LORA + MLA_KV_LORA, MLA_Q_LORA + MLA_KV_LORA + MLA_ROPE], axis=-1)
    q = (_rmsnorm(q_a, q_a_norm_g) @ w_q_b).reshape(b, s, MLA_HEADS, MLA_NOPE + MLA_ROPE)
    q_nope = q[..., :MLA_NOPE]
    q_pe = _rope(q[..., MLA_NOPE:], positions)
    k_pe = _rope(k_pe[:, :, None, :], positions)[:, :, 0]
    kv = (_rmsnorm(c_kv, kv_a_norm_g) @ w_kv_b).reshape(b, s, MLA_HEADS, MLA_NOPE + MLA_V)
    k_nope, v = kv[..., :MLA_NOPE], kv[..., MLA_NOPE:]
    scale = (MLA_NOPE + MLA_ROPE) ** -0.5
    f32 = jnp.float32

    def block(args):
        qn, qp = args
        sc = (jnp.einsum('bqhd,bkhd->bhqk', qn, k_nope, preferred_element_type=f32)
              + jnp.einsum('bqhr,bkr->bhqk', qp, k_pe, preferred_element_type=f32)) * scale
        p = jax.nn.softmax(sc, axis=-1)
        return jnp.einsum('bhqk,bkhd->bqhd', p.astype(v.dtype), v)

    o = _from_blocks(lax.map(block, (_to_blocks(q_nope), _to_blocks(q_pe))))
    o = o.reshape(b, s, MLA_WIDTH) * jax.nn.silu(z)
    return o @ w_out


def setup_inputs(seed: int = 0) -> dict:
    key = jax.random.key(seed)
    ks = jax.random.split(key, 24)
    f32 = jnp.float32
    nrm = lambda k, shape, fan_in: jax.random.normal(k, shape, f32) * (fan_in ** -0.5)
    gain = lambda k, n: 1.0 + 0.02 * jax.random.normal(k, (n,), f32)
    D = D_MODEL
    x = jax.random.normal(ks[0], (BATCH, SEQ, D), f32)
    c = jax.random.normal(ks[1], (BATCH, D), f32)
    offs = jax.random.randint(ks[2], (BATCH, 1), 0, MAX_POS_OFFSET, dtype=jnp.int32)
    positions = offs + jnp.arange(SEQ, dtype=jnp.int32)[None, :]
    return {
        "x": x,
        "c": c,
        "positions": positions,
        "ada_w0": nrm(ks[3], (D, 3 * D), D),
        "ada_b0": 0.02 * jax.random.normal(ks[4], (3 * D,), f32),
        "norm_g0": gain(ks[5], D),
        "w_in0": nrm(ks[6], (D, 4 * DA_WIDTH), D),
        "lam_q1": 0.1 * jax.random.normal(ks[7], (DA_QK_DIM,), f32),
        "lam_k1": 0.1 * jax.random.normal(ks[8], (DA_QK_DIM,), f32),
        "lam_q2": 0.1 * jax.random.normal(ks[9], (DA_QK_DIM,), f32),
        "lam_k2": 0.1 * jax.random.normal(ks[10], (DA_QK_DIM,), f32),
        "subln_g": gain(ks[11], DA_V_DIM),
        "w_out0": nrm(ks[12], (DA_WIDTH, D), DA_WIDTH),
        "ada_w1": nrm(ks[13], (D, 3 * D), D),
        "ada_b1": 0.02 * jax.random.normal(ks[14], (3 * D,), f32),
        "norm_g1": gain(ks[15], D),
        "w_in1": nrm(ks[16], (D, MLA_IN), D),
        "q_a_norm_g": gain(ks[17], MLA_Q_LORA),
        "w_q_b": nrm(ks[18], (MLA_Q_LORA, MLA_HEADS * (MLA_NOPE + MLA_ROPE)), MLA_Q_LORA),
        "kv_a_norm_g": gain(ks[19], MLA_KV_LORA),
        "w_kv_b": nrm(ks[20], (MLA_KV_LORA, MLA_HEADS * (MLA_NOPE + MLA_V)), MLA_KV_LORA),
        "w_out1": nrm(ks[21], (MLA_WIDTH, D), MLA_WIDTH),
        "final_norm_g": gain(ks[22], D),
    }


def reference(x, c, positions,
              ada_w0, ada_b0, norm_g0, w_in0, lam_q1, lam_k1, lam_q2, lam_k2, subln_g, w_out0,
              ada_w1, ada_b1, norm_g1, w_in1, q_a_norm_g, w_q_b, kv_a_norm_g, w_kv_b, w_out1,
              final_norm_g):
    ada_w = (ada_w0, ada_w1)
    ada_b = (ada_b0, ada_b1)
    norm_g = (norm_g0, norm_g1)
    for i in range(DEPTH):
        shift, scale, gate = _modulation(c, ada_w[i], ada_b[i])
        h = _rmsnorm(x, norm_g[i]) * (1.0 + scale) + shift
        if i % N_MIXERS == 0:
            y = _diff_attention(h, positions, w_in0, lam_q1, lam_k1, lam_q2, lam_k2,
                                subln_g, w_out0, _lambda_init(i))
        else:
            y = _mla(h, positions, w_in1, q_a_norm_g, w_q_b, kv_a_norm_g, w_kv_b, w_out1)
        x = x + gate * y
    return _rmsnorm(x, final_norm_g)
```

```python
import functools
import math

import jax
import jax.numpy as jnp
from jax import lax
from jax.experimental import pallas as pl
from jax.experimental.pallas import tpu as pltpu

F32 = jnp.float32
BF16 = jnp.bfloat16

RMS_EPS = 1e-6
ROPE_THETA = 10000.0
LOG2E = math.log2(math.e)

LANES = 128
ROPE_DIM = 64
ROW_TILE = 512
KEY_CHUNK = 512
SCORE_COLS = 512
VMEM_LIMIT = 56 * 1024 * 1024


def _cparams(n_axes):
    return pltpu.CompilerParams(
        dimension_semantics=("arbitrary",) * n_axes, vmem_limit_bytes=VMEM_LIMIT)


def _silu(t):
    return t * jax.nn.sigmoid(t)


def _rms(t, g):
    return t * lax.rsqrt(jnp.mean(t * t, axis=-1, keepdims=True) + RMS_EPS) * g


def _rope_slabs(t, cos, sin_signed):
    first_half = (lax.broadcasted_iota(jnp.int32, (1, LANES), 1) % ROPE_DIM) < (ROPE_DIM // 2)
    outs = []
    for j in range(t.shape[1] // LANES):
        tj = t[:, j * LANES:(j + 1) * LANES]
        partner = jnp.where(first_half,
                            pltpu.roll(tj, LANES - ROPE_DIM // 2, 1),
                            pltpu.roll(tj, ROPE_DIM // 2, 1))
        outs.append(tj * cos + partner * sin_signed)
    return outs


def _mod_kernel(c_ref, w0_ref, b0_ref, w1_ref, b1_ref, m0_ref, m1_ref):
    cs = _silu(c_ref[...])
    m0_ref[...] = jnp.dot(cs, w0_ref[...], preferred_element_type=F32) + b0_ref[...]
    m1_ref[...] = jnp.dot(cs, w1_ref[...], preferred_element_type=F32) + b1_ref[...]


def _modulation(c8, w0, b0, w1, b1):
    d = c8.shape[1]
    n = w0.shape[1]
    tn = d
    wspec = pl.BlockSpec((d, tn), lambda j: (0, j))
    bspec = pl.BlockSpec((1, tn), lambda j: (0, j))
    ospec = pl.BlockSpec((8, tn), lambda j: (0, j))
    return pl.pallas_call(
        _mod_kernel,
        grid=(n // tn,),
        in_specs=[pl.BlockSpec((8, d), lambda j: (0, 0)), wspec, bspec, wspec, bspec],
        out_specs=[ospec, ospec],
        out_shape=[jax.ShapeDtypeStruct((8, n), F32)] * 2,
        compiler_params=_cparams(1),
        name="mod",
    )(c8, w0, b0.reshape(1, n), w1, b1.reshape(1, n))


def _rope_table_kernel(pos_ref, inv_ref, sgn_ref, cos_ref, sin_ref):
    ang = pos_ref[...].astype(F32) * inv_ref[...]
    cos_ref[...] = jnp.cos(ang)
    sin_ref[...] = jnp.sin(ang) * sgn_ref[...]


def _rope_tables(positions):
    n = positions.size
    half = ROPE_DIM // 2
    inv = ROPE_THETA ** (-jnp.arange(half, dtype=F32) / half)
    inv_l = jnp.tile(inv, LANES // half).reshape(1, LANES)
    sgn = jnp.where((jnp.arange(LANES) % ROPE_DIM) < half, -1.0, 1.0).astype(F32).reshape(1, LANES)
    tm = 2048
    cspec = pl.BlockSpec((1, LANES), lambda i: (0, 0))
    tspec = pl.BlockSpec((tm, LANES), lambda i: (i, 0))
    return pl.pallas_call(
        _rope_table_kernel,
        grid=(n // tm,),
        in_specs=[pl.BlockSpec((tm, 1), lambda i: (i, 0)), cspec, cspec],
        out_specs=[tspec, tspec],
        out_shape=[jax.ShapeDtypeStruct((n, LANES), F32)] * 2,
        compiler_params=_cparams(1),
        name="rope_tables",
    )(positions.reshape(n, 1), inv_l, sgn)


def _proj0_kernel(x_ref, mod_ref, g_ref, w_ref, cos_ref, sin_ref,
                  qt_ref, k_ref, vt_ref, z_ref, *, d, width):
    x = x_ref[...]
    mod = mod_ref[...]
    h = _rms(x, g_ref[...]) * (1.0 + mod[:, d:2 * d]) + mod[:, :d]
    proj = jnp.dot(h.astype(BF16), w_ref[...], preferred_element_type=F32)
    cos = cos_ref[...]
    sin = sin_ref[...]
    q = jnp.concatenate(_rope_slabs(proj[:, :width], cos, sin), axis=1)
    k = jnp.concatenate(_rope_slabs(proj[:, width:2 * width], cos, sin), axis=1)
    qt_ref[...] = q.T.astype(BF16)
    k_ref[...] = k.astype(BF16)
    vt_ref[...] = proj[:, 2 * width:3 * width].T.astype(BF16)
    z_ref[...] = proj[:, 3 * width:]


def _proj0(x, mod0, norm_g, w_in, cos_t, sin_t):
    b, s, d = x.shape
    width = w_in.shape[1] // 4
    tm = ROW_TILE
    nt = s // tm
    row = lambda bi, i: (bi, i, 0)
    colt = lambda bi, i: (bi, 0, i)
    return pl.pallas_call(
        functools.partial(_proj0_kernel, d=d, width=width),
        grid=(b, nt),
        in_specs=[
            pl.BlockSpec((None, tm, d), row),
            pl.BlockSpec((None, 1, 3 * d), lambda bi, i: (bi, 0, 0)),
            pl.BlockSpec((1, d), lambda bi, i: (0, 0)),
            pl.BlockSpec((d, 4 * width), lambda bi, i: (0, 0)),
            pl.BlockSpec((tm, LANES), lambda bi, i: (bi * nt + i, 0)),
            pl.BlockSpec((tm, LANES), lambda bi, i: (bi * nt + i, 0)),
        ],
        out_specs=[
            pl.BlockSpec((None, width, tm), colt),
            pl.BlockSpec((None, tm, width), row),
            pl.BlockSpec((None, width, tm), colt),
            pl.BlockSpec((None, tm, width), row),
        ],
        out_shape=[
            jax.ShapeDtypeStruct((b, width, s), BF16),
            jax.ShapeDtypeStruct((b, s, width), BF16),
            jax.ShapeDtypeStruct((b, width, s), BF16),
            jax.ShapeDtypeStruct((b, s, width), F32),
        ],
        compiler_params=_cparams(2),
        name="proj0",
    )(x, mod0, norm_g.reshape(1, d), w_in, cos_t, sin_t)


def _attention_core(k_ref, vt_ref, w_scr, s_scr, p_scr, exp_scale):
    n_keys = k_ref.shape[0]
    cols = w_scr.shape[1]
    kc = KEY_CHUNK
    n_chunks = n_keys // kc

    def qk_body(j, mx):
        r = pl.multiple_of(j * kc, kc)
        s = jnp.dot(k_ref[pl.ds(r, kc), :], w_scr[...], preferred_element_type=F32)
        s_scr[pl.ds(r, kc), :] = s
        return jnp.maximum(mx, jnp.max(s.reshape(kc // 8, 8, cols), axis=0))

    mx = lax.fori_loop(0, n_chunks, qk_body, jnp.full((8, cols), -jnp.inf, F32))
    mc = jnp.max(mx, axis=0, keepdims=True) * exp_scale

    def sm_body(j, ls):
        r = pl.multiple_of(j * kc, kc)
        p = jnp.exp2(s_scr[pl.ds(r, kc), :] * exp_scale - mc)
        p_scr[pl.ds(r, kc), :] = p.astype(BF16)
        return ls + jnp.sum(p.reshape(kc // 8, 8, cols), axis=0)

    ls = lax.fori_loop(0, n_chunks, sm_body, jnp.zeros((8, cols), F32))
    l = jnp.sum(ls, axis=0, keepdims=True)
    ot = jnp.dot(vt_ref[...], p_scr[...], preferred_element_type=F32)
    return ot, l


def _diff_attn_kernel(k_ref, vt_ref, qt_ref, lam_ref, g_ref, o_ref, w_scr, s_scr, p_scr,
                      *, lambda_init):
    dq = ROPE_DIM
    qc = qt_ref.shape[1]
    w_scr[...] = jnp.zeros_like(w_scr)
    w_scr[0:dq, 0:qc] = qt_ref[0:dq, :]
    w_scr[dq:2 * dq, qc:2 * qc] = qt_ref[dq:2 * dq, :]
    ot, l = _attention_core(k_ref, vt_ref, w_scr, s_scr, p_scr, (dq ** -0.5) * LOG2E)
    lam_v = lam_ref[...]
    lam = (jnp.exp(jnp.sum(lam_v[0:1] * lam_v[1:2], axis=1, keepdims=True))
           - jnp.exp(jnp.sum(lam_v[2:3] * lam_v[3:4], axis=1, keepdims=True)) + lambda_init)
    o = ot[:, :qc] / l[:, :qc] - lam * (ot[:, qc:] / l[:, qc:])
    o = o * lax.rsqrt(jnp.mean(o * o, axis=0, keepdims=True) + RMS_EPS) * g_ref[...]
    o_ref[...] = (o * (1.0 - lambda_init)).T


def _mla_attn_kernel(k_ref, vt_ref, qn_ref, qp_ref, o_ref, w_scr, s_scr, p_scr, *, scale):
    dn = qn_ref.shape[0]
    dp = qp_ref.shape[0]
    w_scr[0:dn, :] = qn_ref[...]
    w_scr[dn:dn + dp, :] = qp_ref[...]
    w_scr[dn + dp:, :] = jnp.zeros((w_scr.shape[0] - dn - dp, w_scr.shape[1]), BF16)
    ot, l = _attention_core(k_ref, vt_ref, w_scr, s_scr, p_scr, scale * LOG2E)
    o_ref[...] = (ot / l).T


def _attn_scratch(dk, s):
    return [pltpu.VMEM((dk, SCORE_COLS), BF16),
            pltpu.VMEM((s, SCORE_COLS), F32),
            pltpu.VMEM((s, SCORE_COLS), BF16)]


def _diff_attention(k, vt, qt, lam_vecs, subln_g, heads, lambda_init):
    b, s, width = k.shape
    dv = width // heads
    qc = SCORE_COLS // 2
    return pl.pallas_call(
        functools.partial(_diff_attn_kernel, lambda_init=lambda_init),
        grid=(b, heads, s // qc),
        in_specs=[
            pl.BlockSpec((None, s, dv), lambda bi, h, i: (bi, 0, h)),
            pl.BlockSpec((None, dv, s), lambda bi, h, i: (bi, h, 0)),
            pl.BlockSpec((None, dv, qc), lambda bi, h, i: (bi, h, i)),
            pl.BlockSpec((4, ROPE_DIM), lambda bi, h, i: (0, 0)),
            pl.BlockSpec((dv, 1), lambda bi, h, i: (0, 0)),
        ],
        out_specs=pl.BlockSpec((None, qc, dv), lambda bi, h, i: (bi, i, h)),
        out_shape=jax.ShapeDtypeStruct((b, s, width), F32),
        scratch_shapes=_attn_scratch(dv, s),
        compiler_params=_cparams(3),
        name="diff_attn",
    )(k, vt, qt, lam_vecs, subln_g.reshape(dv, 1))


def _mla_attention(kfull, vt, qnt, qpt, heads, scale):
    b, s, kw = kfull.shape
    dk = kw // heads
    dv = vt.shape[1] // heads
    dn = qnt.shape[1] // heads
    dp = qpt.shape[1] // heads
    qc = SCORE_COLS
    return pl.pallas_call(
        functools.partial(_mla_attn_kernel, scale=scale),
        grid=(b, heads, s // qc),
        in_specs=[
            pl.BlockSpec((None, s, dk), lambda bi, h, i: (bi, 0, h)),
            pl.BlockSpec((None, dv, s), lambda bi, h, i: (bi, h, 0)),
            pl.BlockSpec((None, dn, qc), lambda bi, h, i: (bi, h, i)),
            pl.BlockSpec((None, dp, qc), lambda bi, h, i: (bi, h, i)),
        ],
        out_specs=pl.BlockSpec((None, qc, dv), lambda bi, h, i: (bi, i, h)),
        out_shape=jax.ShapeDtypeStruct((b, s, heads * dv), F32),
        scratch_shapes=_attn_scratch(dk, s),
        compiler_params=_cparams(3),
        name="mla_attn",
    )(kfull, vt, qnt, qpt)


def _mid_kernel(o_ref, z_ref, x_ref, mod0_ref, wo_ref, mod1_ref, g1_ref, win_ref,
                qg_ref, wq_ref, kvg_ref, wkv_ref, cos_ref, sin_ref,
                x1_ref, qnt_ref, qpt_ref, kf_ref, vt_ref, z1_ref,
                *, d, q_lora, kv_lora, heads, nope):
    y = jnp.dot((o_ref[...] * _silu(z_ref[...])).astype(BF16), wo_ref[...],
                preferred_element_type=F32)
    x1 = x_ref[...] + mod0_ref[...][:, 2 * d:] * y
    x1_ref[...] = x1
    mod1 = mod1_ref[...]
    h = _rms(x1, g1_ref[...]) * (1.0 + mod1[:, d:2 * d]) + mod1[:, :d]
    proj = jnp.dot(h.astype(BF16), win_ref[...], preferred_element_type=F32)
    z_off = q_lora + kv_lora
    pe_off = z_off + d
    z1_ref[...] = proj[:, z_off:pe_off]
    cos = cos_ref[...]
    sin = sin_ref[...]
    qa = _rms(proj[:, :q_lora], qg_ref[...])
    q = jnp.dot(qa.astype(BF16), wq_ref[...], preferred_element_type=F32)
    qnt_ref[...] = q[:, :heads * nope].T.astype(BF16)
    qp = jnp.concatenate(_rope_slabs(q[:, heads * nope:], cos, sin), axis=1)
    qpt_ref[...] = qp.T.astype(BF16)
    ckv = _rms(proj[:, q_lora:z_off], kvg_ref[...])
    kv = jnp.dot(ckv.astype(BF16), wkv_ref[...], preferred_element_type=F32)
    kpe = _rope_slabs(proj[:, pe_off:pe_off + LANES], cos, sin)[0].astype(BF16)
    for hd in range(heads):
        kf_ref[:, 2 * hd * nope:(2 * hd + 1) * nope] = kv[:, hd * nope:(hd + 1) * nope].astype(BF16)
        kf_ref[:, (2 * hd + 1) * nope:(2 * hd + 2) * nope] = kpe
    vt_ref[...] = kv[:, heads * nope:].T.astype(BF16)


def _mid(o, z, x, mod0, w_out, mod1, norm_g, w_in, qg, w_q, kvg, w_kv, cos_t, sin_t,
         heads, nope):
    b, s, d = x.shape
    q_lora = w_q.shape[0]
    kv_lora = w_kv.shape[0]
    tm = ROW_TILE
    nt = s // tm
    row = lambda bi, i: (bi, i, 0)
    colt = lambda bi, i: (bi, 0, i)
    full = lambda bi, i: (0, 0)
    modspec = pl.BlockSpec((None, 1, 3 * d), lambda bi, i: (bi, 0, 0))
    tspec = pl.BlockSpec((tm, LANES), lambda bi, i: (bi * nt + i, 0))
    nq = w_q.shape[1]
    n_nope = heads * nope
    return pl.pallas_call(
        functools.partial(_mid_kernel, d=d, q_lora=q_lora, kv_lora=kv_lora, heads=heads, nope=nope),
        grid=(b, nt),
        in_specs=[
            pl.BlockSpec((None, tm, d), row),
            pl.BlockSpec((None, tm, d), row),
            pl.BlockSpec((None, tm, d), row),
            modspec,
            pl.BlockSpec(w_out.shape, full),
            modspec,
            pl.BlockSpec((1, d), full),
            pl.BlockSpec(w_in.shape, full),
            pl.BlockSpec((1, q_lora), full),
            pl.BlockSpec(w_q.shape, full),
            pl.BlockSpec((1, kv_lora), full),
            pl.BlockSpec(w_kv.shape, full),
            tspec, tspec,
        ],
        out_specs=[
            pl.BlockSpec((None, tm, d), row),
            pl.BlockSpec((None, n_nope, tm), colt),
            pl.BlockSpec((None, nq - n_nope, tm), colt),
            pl.BlockSpec((None, tm, 2 * n_nope), row),
            pl.BlockSpec((None, n_nope, tm), colt),
            pl.BlockSpec((None, tm, d), row),
        ],
        out_shape=[
            jax.ShapeDtypeStruct((b, s, d), F32),
            jax.ShapeDtypeStruct((b, n_nope, s), BF16),
            jax.ShapeDtypeStruct((b, nq - n_nope, s), BF16),
            jax.ShapeDtypeStruct((b, s, 2 * n_nope), BF16),
            jax.ShapeDtypeStruct((b, n_nope, s), BF16),
            jax.ShapeDtypeStruct((b, s, d), F32),
        ],
        compiler_params=_cparams(2),
        name="mid",
    )(o, z, x, mod0, w_out, mod1, norm_g.reshape(1, d), w_in, qg.reshape(1, q_lora), w_q,
      kvg.reshape(1, kv_lora), w_kv, cos_t, sin_t)


def _final_kernel(o_ref, z_ref, x_ref, mod_ref, wo_ref, g_ref, out_ref, *, d):
    y = jnp.dot((o_ref[...] * _silu(z_ref[...])).astype(BF16), wo_ref[...],
                preferred_element_type=F32)
    x2 = x_ref[...] + mod_ref[...][:, 2 * d:] * y
    out_ref[...] = _rms(x2, g_ref[...])


def _final(o, z, x, mod1, w_out, g):
    b, s, d = x.shape
    tm = ROW_TILE
    row = lambda bi, i: (bi, i, 0)
    rspec = pl.BlockSpec((None, tm, d), row)
    return pl.pallas_call(
        functools.partial(_final_kernel, d=d),
        grid=(b, s // tm),
        in_specs=[rspec, rspec, rspec,
                  pl.BlockSpec((None, 1, 3 * d), lambda bi, i: (bi, 0, 0)),
                  pl.BlockSpec(w_out.shape, lambda bi, i: (0, 0)),
                  pl.BlockSpec((1, d), lambda bi, i: (0, 0))],
        out_specs=rspec,
        out_shape=jax.ShapeDtypeStruct((b, s, d), F32),
        compiler_params=_cparams(2),
        name="final",
    )(o, z, x, mod1, w_out, g.reshape(1, d))


def kernel(x, c, positions, ada_w0, ada_b0, norm_g0, w_in0, lam_q1, lam_k1, lam_q2, lam_k2, subln_g, w_out0, ada_w1, ada_b1, norm_g1, w_in1, q_a_norm_g, w_q_b, kv_a_norm_g, w_kv_b, w_out1, final_norm_g):
    b, s, d = x.shape
    da_heads = w_in0.shape[1] // 4 // subln_g.shape[0]
    q_lora = q_a_norm_g.shape[0]
    kv_lora = kv_a_norm_g.shape[0]
    mla_heads = da_heads
    mla_v = w_out1.shape[0] // mla_heads
    nope = w_kv_b.shape[1] // mla_heads - mla_v
    rope = w_q_b.shape[1] // mla_heads - nope
    assert rope == ROPE_DIM and lam_q1.shape[0] == ROPE_DIM

    c8 = jnp.pad(c, ((0, 8 - b), (0, 0)))
    mod0, mod1 = _modulation(c8, ada_w0, ada_b0, ada_w1, ada_b1)
    mod0 = mod0[:b].reshape(b, 1, 3 * d)
    mod1 = mod1[:b].reshape(b, 1, 3 * d)
    cos_t, sin_t = _rope_tables(positions)

    qt, k, vt, z0 = _proj0(x, mod0, norm_g0, w_in0.astype(BF16), cos_t, sin_t)
    lam_vecs = jnp.stack([lam_q1, lam_k1, lam_q2, lam_k2])
    lambda_init = 0.8 - 0.6 * math.exp(-0.3 * 0)
    o0 = _diff_attention(k, vt, qt, lam_vecs, subln_g, da_heads, lambda_init)

    z_cols = w_in1.shape[1] - q_lora - kv_lora - rope
    w_in1_r = jnp.concatenate(
        [w_in1[:, :q_lora + kv_lora], w_in1[:, q_lora + kv_lora + rope:],
         w_in1[:, q_lora + kv_lora:q_lora + kv_lora + rope],
         jnp.zeros((d, LANES - rope), w_in1.dtype)], axis=1).astype(BF16)
    assert z_cols == d
    wq = w_q_b.reshape(q_lora, mla_heads, nope + rope)
    w_q_r = jnp.concatenate([wq[:, :, :nope].reshape(q_lora, -1),
                             wq[:, :, nope:].reshape(q_lora, -1)], axis=1).astype(BF16)
    wkv = w_kv_b.reshape(kv_lora, mla_heads, nope + mla_v)
    w_kv_r = jnp.concatenate([wkv[:, :, :nope].reshape(kv_lora, -1),
                              wkv[:, :, nope:].reshape(kv_lora, -1)], axis=1).astype(BF16)

    x1, qnt, qpt, kfull, vt1, z1 = _mid(
        o0, z0, x, mod0, w_out0.astype(BF16), mod1, norm_g1, w_in1_r, q_a_norm_g, w_q_r,
        kv_a_norm_g, w_kv_r, cos_t, sin_t, mla_heads, nope)
    o1 = _mla_attention(kfull, vt1, qnt, qpt, mla_heads, (nope + rope) ** -0.5)
    return _final(o1, z1, x1, mod1, w_out1.astype(BF16), final_norm_g)
```
